```python
import jax, jax.numpy as jnp
from jax import lax
import numpy as np

D_MODEL = 1024
BATCH = 8
SEQ = 4096
DEPTH = 4

PLE_DIM = 256
D_FF = 2816
POOL_WINDOWS = (2, 4, 8, 16)
POOL_GROUP = D_MODEL // 8
POOL_WIDTH = len(POOL_WINDOWS) * POOL_GROUP
SGU_HEADS = 4
SGU_HEAD_DIM = D_MODEL // 8
SGU_WIDTH = SGU_HEADS * SGU_HEAD_DIM
CHUNK = 128
CONV_WIDTH = D_MODEL // 2
CONV_KERNEL = 31
N_BRANCH = 3
OFF_POOL = 0
OFF_U = OFF_POOL + POOL_WIDTH
OFF_V = OFF_U + SGU_WIDTH
OFF_GLU_A = OFF_V + SGU_WIDTH
OFF_GLU_B = OFF_GLU_A + CONV_WIDTH
OFF_GATES = OFF_GLU_B + CONV_WIDTH
IN_COLS = OFF_GATES + N_BRANCH * D_MODEL
EPS = 1e-6

kernel_name = "hybrid_pool_sgu_conformer_gated_trunk"


def rms_norm(x, g):
    xf = x.astype(jnp.float32)
    y = xf * lax.rsqrt(jnp.mean(xf * xf, axis=-1, keepdims=True) + EPS)
    return (y * g.astype(jnp.float32)).astype(x.dtype)


def layer_norm(x, g, b):
    xf = x.astype(jnp.float32)
    mu = jnp.mean(xf, axis=-1, keepdims=True)
    xc = xf - mu
    y = xc * lax.rsqrt(jnp.mean(xc * xc, axis=-1, keepdims=True) + EPS)
    return (y * g.astype(jnp.float32) + b.astype(jnp.float32)).astype(x.dtype)


def swiglu(x, w_gate, w_up, w_down):
    return (jax.nn.silu(x @ w_gate) * (x @ w_up)) @ w_down


def pool_mixer(xa, w_grp, scale):
    b_, s_, _ = xa.shape
    xf = xa.astype(jnp.float32).reshape(b_, s_, len(POOL_WINDOWS), POOL_GROUP)
    csum = jnp.cumsum(xf, axis=1)
    pos = jnp.arange(s_)
    outs = []
    for gi, w in enumerate(POOL_WINDOWS):
        cg = csum[:, :, gi]
        shifted = jnp.pad(cg, ((0, 0), (w, 0), (0, 0)))[:, :s_]
        count = jnp.minimum(pos + 1, w).astype(jnp.float32)[None, :, None]
        outs.append((cg - shifted) / count - xf[:, :, gi])
    pooled = jnp.stack(outs, axis=2).astype(xa.dtype)
    y = jnp.einsum('bsgi,gio->bsgo', pooled, w_grp)
    return y.reshape(b_, s_, POOL_WIDTH) * scale


def sgu_mixer(u, v, ln_g, ln_b, w_s, b_s):
    b_, s_, _ = u.shape
    u = jax.nn.gelu(u, approximate=False)
    v = layer_norm(jax.nn.gelu(v, approximate=False), ln_g, ln_b)
    vc = v.reshape(b_, s_ // CHUNK, CHUNK, SGU_HEADS, SGU_HEAD_DIM)
    causal = jnp.tril(jnp.ones((CHUNK, CHUNK), dtype=bool))
    w = jnp.where(causal[None], w_s, jnp.zeros_like(w_s))
    s = jnp.einsum('hts,bcshd->bcthd', w, vc) + b_s.T[None, None, :, :, None]
    return u * s.reshape(b_, s_, SGU_WIDTH)


def conv_mixer(a, gate, dw_k, dw_b, ln_g, ln_b):
    xg = a * jax.nn.sigmoid(gate)
    y = lax.conv_general_dilated(
        xg, dw_k, window_strides=(1,), padding=[(CONV_KERNEL - 1, 0)],
        dimension_numbers=('NWC', 'WIO', 'NWC'), feature_group_count=CONV_WIDTH) + dw_b
    return jax.nn.silu(layer_norm(y, ln_g, ln_b))


def setup_inputs(seed: int = 0) -> dict:
    key = jax.random.key(seed)
    ks = iter(jax.random.split(key, 40))
    L = DEPTH

    def w(shape, fan_in):
        return jax.random.normal(next(ks), shape, jnp.float32) * (fan_in ** -0.5)

    def gain(shape):
        return 1.0 + 0.05 * jax.random.normal(next(ks), shape, jnp.float32)

    def bias(shape):
        return 0.02 * jax.random.normal(next(ks), shape, jnp.float32)

    return {
        "x": jax.random.normal(next(ks), (BATCH, SEQ, D_MODEL), jnp.float32),
        "p": jax.random.normal(next(ks), (DEPTH, BATCH, SEQ, PLE_DIM), jnp.float32),
        "ffn1_pre_g": gain((L, D_MODEL)),
        "ffn1_w_gate": w((L, D_MODEL, D_FF), D_MODEL),
        "ffn1_w_up": w((L, D_MODEL, D_FF), D_MODEL),
        "ffn1_w_down": w((L, D_FF, D_MODEL), D_FF),
        "ffn1_post_g": gain((L, D_MODEL)),
        "mix_pre_g": gain((L, D_MODEL)),
        "w_in": w((L, D_MODEL, IN_COLS), D_MODEL),
        "pool_w": w((L, len(POOL_WINDOWS), POOL_GROUP, POOL_GROUP), POOL_GROUP),
        "pool_scale": gain((L, POOL_WIDTH)),
        "w_pool_out": w((L, POOL_WIDTH, D_MODEL), POOL_WIDTH),
        "sgu_ln_g": gain((L, SGU_WIDTH)),
        "sgu_ln_b": bias((L, SGU_WIDTH)),
        "sgu_w_s": w((L, SGU_HEADS, CHUNK, CHUNK), CHUNK),
        "sgu_b_s": 1.0 + 0.1 * jax.random.normal(next(ks), (L, SGU_HEADS, CHUNK), jnp.float32),
        "w_sgu_out": w((L, SGU_WIDTH, D_MODEL), SGU_WIDTH),
        "conv_dw_k": w((L, CONV_KERNEL, 1, CONV_WIDTH), CONV_KERNEL),
        "conv_dw_b": bias((L, CONV_WIDTH)),
        "conv_ln_g": gain((L, CONV_WIDTH)),
        "conv_ln_b": bias((L, CONV_WIDTH)),
        "w_conv_out": w((L, CONV_WIDTH, D_MODEL), CONV_WIDTH),
        "w_out": w((L, D_MODEL, D_MODEL), D_MODEL),
        "mix_post_g": gain((L, D_MODEL)),
        "ffn2_pre_g": gain((L, D_MODEL)),
        "ffn2_w_gate": w((L, D_MODEL, D_FF), D_MODEL),
        "ffn2_w_up": w((L, D_MODEL, D_FF), D_MODEL),
        "ffn2_w_down": w((L, D_FF, D_MODEL), D_FF),
        "ffn2_post_g": gain((L, D_MODEL)),
        "ple_w_proj": w((L, PLE_DIM, D_MODEL), PLE_DIM),
        "ple_pre_g": gain((L, D_MODEL)),
        "ple_w_gate": w((L, D_MODEL, D_MODEL), D_MODEL),
        "ple_post_g": gain((L, D_MODEL)),
    }


def reference(x, p, ffn1_pre_g, ffn1_w_gate, ffn1_w_up, ffn1_w_down, ffn1_post_g,
              mix_pre_g, w_in, pool_w, pool_scale, w_pool_out,
              sgu_ln_g, sgu_ln_b, sgu_w_s, sgu_b_s, w_sgu_out,
              conv_dw_k, conv_dw_b, conv_ln_g, conv_ln_b, w_conv_out,
              w_out, mix_post_g,
              ffn2_pre_g, ffn2_w_gate, ffn2_w_up, ffn2_w_down, ffn2_post_g,
              ple_w_proj, ple_pre_g, ple_w_gate, ple_post_g):
    h = x
    b_, s_, _ = x.shape
    for i in range(DEPTH):
        f = swiglu(rms_norm(h, ffn1_pre_g[i]), ffn1_w_gate[i], ffn1_w_up[i], ffn1_w_down[i])
        h = h + 0.5 * rms_norm(f, ffn1_post_g[i])

        n = rms_norm(h, mix_pre_g[i])
        z = n @ w_in[i]
        z_pool = z[..., OFF_POOL:OFF_U]
        z_u = z[..., OFF_U:OFF_V]
        z_v = z[..., OFF_V:OFF_GLU_A]
        z_a = z[..., OFF_GLU_A:OFF_GLU_B]
        z_b = z[..., OFF_GLU_B:OFF_GATES]
        gates = jax.nn.sigmoid(z[..., OFF_GATES:]).reshape(b_, s_, N_BRANCH, D_MODEL)

        y_pool = pool_mixer(z_pool, pool_w[i], pool_scale[i]) @ w_pool_out[i]
        y_sgu = sgu_mixer(z_u, z_v, sgu_ln_g[i], sgu_ln_b[i], sgu_w_s[i], sgu_b_s[i]) @ w_sgu_out[i]
        y_conv = conv_mixer(z_a, z_b, conv_dw_k[i], conv_dw_b[i],
                            conv_ln_g[i], conv_ln_b[i]) @ w_conv_out[i]
        merged = gates[:, :, 0] * y_pool + gates[:, :, 1] * y_sgu + gates[:, :, 2] * y_conv
        h = h + rms_norm(merged @ w_out[i], mix_post_g[i])

        f = swiglu(rms_norm(h, ffn2_pre_g[i]), ffn2_w_gate[i], ffn2_w_up[i], ffn2_w_down[i])
        h = h + 0.5 * rms_norm(f, ffn2_post_g[i])

        e = p[i] @ ple_w_proj[i]
        g = jax.nn.sigmoid(rms_norm(h, ple_pre_g[i]) @ ple_w_gate[i])
        h = h + rms_norm(g * e, ple_post_g[i])
    return h
```

```python
import functools
import math

import jax
import jax.numpy as jnp
from jax import lax
from jax.experimental import pallas as pl
from jax.experimental.pallas import tpu as pltpu

POOL_WINDOWS = (2, 4, 8, 16)
GROUP = 128
CHUNK = 128
CONV_KERNEL = 31
EPS = 1e-6

SUBLANES = 8
POOL_HALO = 16
CONV_HALO = 32
ROW_TILE = 512
FF_CHUNK = 512
VMEM_LIMIT_BYTES = 56 * 1024 * 1024

_BF16 = jnp.bfloat16
_F32 = jnp.float32


def _mm(a, w):
    return jnp.dot(a.astype(_BF16), w, preferred_element_type=_F32)


def _rms_norm(x, g):
    return x * lax.rsqrt(jnp.mean(x * x, axis=-1, keepdims=True) + EPS) * g


def _layer_norm(x, g, b):
    xc = x - jnp.mean(x, axis=-1, keepdims=True)
    return xc * lax.rsqrt(jnp.mean(xc * xc, axis=-1, keepdims=True) + EPS) * g + b


def _sigmoid(x):
    return 1.0 / (1.0 + jnp.exp(-x))


def _gelu(x):
    return 0.5 * x * (1.0 + lax.erf(x * math.sqrt(0.5)))


def _swiglu_half_step(h, pre_g, wg_ref, wu_ref, wd_ref, post_g):
    x = _rms_norm(h, pre_g).astype(_BF16)
    d_ff = wg_ref.shape[1]
    acc = jnp.zeros(h.shape, _F32)
    for c0 in range(0, d_ff, FF_CHUNK):
        c1 = min(c0 + FF_CHUNK, d_ff)
        g = _mm(x, wg_ref[:, c0:c1])
        u = _mm(x, wu_ref[:, c0:c1])
        acc = acc + _mm(g * _sigmoid(g) * u, wd_ref[c0:c1, :])
    return h + 0.5 * _rms_norm(acc, post_g)


def _ffn_kernel(h_ref, pre_g, wg, wu, wd, post_g, o_ref):
    o_ref[...] = _swiglu_half_step(h_ref[...], pre_g[...], wg, wu, wd, post_g[...])


def _ffn_ple_kernel(h_ref, p_ref, pre_g, wg, wu, wd, post_g,
                    w_proj, ple_pre_g, w_gate, ple_post_g, o_ref):
    h = _swiglu_half_step(h_ref[...], pre_g[...], wg, wu, wd, post_g[...])
    e = _mm(p_ref[...], w_proj[...])
    g = _sigmoid(_mm(_rms_norm(h, ple_pre_g[...]), w_gate[...]))
    o_ref[...] = h + _rms_norm(g * e, ple_post_g[...])


def _shifted_rows(buf_ref, halo, row0, nrows, cols, max_shift):
    reach = (max_shift // SUBLANES) * SUBLANES
    loaded = {}

    def tap(s):
        r, q8 = s % SUBLANES, (s // SUBLANES) * SUBLANES
        if r not in loaded:
            start = halo + row0 - r - reach
            loaded[r] = buf_ref[start:start + nrows + reach, cols]
        return loaded[r][reach - q8:reach - q8 + nrows]

    return tap


def _mixer_kernel(h_ref, pre_g, w_in, pool_w, pool_scale, w_pool_out,
                  sgu_ln_g, sgu_ln_b, sgu_w_s, sgu_b_exp, w_sgu_out,
                  dw_k, dw_b, conv_ln_g, conv_ln_b, w_conv_out, w_out, post_g,
                  o_ref, pool_buf, conv_buf, conv_out, *, tiles_per_seq):
    tm = h_ref.shape[0]
    width = GROUP * len(POOL_WINDOWS)
    seq_tile = pl.program_id(0) % tiles_per_seq

    @pl.when(seq_tile == 0)
    def _():
        pool_buf[0:POOL_HALO, :] = jnp.zeros((POOL_HALO, width), _F32)
        conv_buf[0:CONV_HALO, :] = jnp.zeros((CONV_HALO, width), _F32)

    h = h_ref[...]
    n = _rms_norm(h, pre_g[...]).astype(_BF16)
    off = [i * width for i in range(6)]

    def gate(k):
        c0 = off[5] + k * h.shape[1]
        return _sigmoid(_mm(n, w_in[:, c0:c0 + h.shape[1]]))

    pool_buf[POOL_HALO:POOL_HALO + tm, :] = _mm(n, w_in[:, off[0]:off[1]])
    pos = seq_tile * tm + lax.broadcasted_iota(jnp.int32, (CHUNK, GROUP), 0)
    mixed = []
    for gi, w in enumerate(POOL_WINDOWS):
        cols = slice(gi * GROUP, (gi + 1) * GROUP)
        blocks = []
        for row0 in range(0, tm, CHUNK):
            tap = _shifted_rows(pool_buf, POOL_HALO, row0, CHUNK, cols, w - 1)
            total = tap(0)
            for s in range(1, w):
                total = total + tap(s)
            count = jnp.minimum(pos + (row0 + 1), w).astype(_F32)
            blocks.append(total / count - tap(0))
        pooled = jnp.concatenate(blocks, axis=0)
        mixed.append(_mm(pooled, pool_w[gi]))
    pool_buf[0:POOL_HALO, :] = pool_buf[tm:tm + POOL_HALO, :]
    y_pool = _mm(jnp.concatenate(mixed, axis=1) * pool_scale[...], w_pool_out[...])
    merged = gate(0) * y_pool

    u = _gelu(_mm(n, w_in[:, off[1]:off[2]]))
    v = _layer_norm(_gelu(_mm(n, w_in[:, off[2]:off[3]])), sgu_ln_g[...], sgu_ln_b[...])
    v = v.astype(_BF16)
    causal = (lax.broadcasted_iota(jnp.int32, (CHUNK, CHUNK), 0)
              >= lax.broadcasted_iota(jnp.int32, (CHUNK, CHUNK), 1))
    heads = sgu_w_s.shape[0]
    w_s = [jnp.where(causal, sgu_w_s[hd], 0.0).astype(_BF16) for hd in range(heads)]
    bias = sgu_b_exp[...]
    rows = []
    for row0 in range(0, tm, CHUNK):
        rows.append(jnp.concatenate(
            [_mm(w_s[hd], v[row0:row0 + CHUNK, hd * GROUP:(hd + 1) * GROUP])
             for hd in range(heads)], axis=1) + bias)
    y_sgu = _mm(u * jnp.concatenate(rows, axis=0), w_sgu_out[...])
    merged = merged + gate(1) * y_sgu

    a = _mm(n, w_in[:, off[3]:off[4]])
    b = _mm(n, w_in[:, off[4]:off[5]])
    conv_buf[CONV_HALO:CONV_HALO + tm, :] = a * _sigmoid(b)
    for row0 in range(0, tm, CHUNK):
        for c0 in range(0, width, GROUP):
            cols = slice(c0, c0 + GROUP)
            tap = _shifted_rows(conv_buf, CONV_HALO, row0, CHUNK, cols, CONV_KERNEL - 1)
            acc = None
            for k in range(CONV_KERNEL):
                term = tap(CONV_KERNEL - 1 - k) * dw_k[k:k + 1, cols]
                acc = term if acc is None else acc + term
            conv_out[row0:row0 + CHUNK, cols] = acc
    conv_buf[0:CONV_HALO, :] = conv_buf[tm:tm + CONV_HALO, :]
    y = _layer_norm(conv_out[...] + dw_b[...], conv_ln_g[...], conv_ln_b[...])
    y_conv = _mm(y * _sigmoid(y), w_conv_out[...])
    merged = merged + gate(2) * y_conv

    o_ref[...] = h + _rms_norm(_mm(merged, w_out[...]), post_g[...])


def _row_spec(tm, cols):
    return pl.BlockSpec((tm, cols), lambda i: (i, 0))


def _whole_spec(arr):
    zeros = (0,) * arr.ndim
    return pl.BlockSpec(arr.shape, lambda i: zeros)


def _row_tiled_call(body, name, n_rows, tm, row_inputs, params, scratch_shapes=()):
    d_model = row_inputs[0].shape[1]
    return pl.pallas_call(
        body,
        name=name,
        grid=(n_rows // tm,),
        in_specs=[_row_spec(tm, r.shape[1]) for r in row_inputs] + [_whole_spec(p) for p in params],
        out_specs=_row_spec(tm, d_model),
        out_shape=jax.ShapeDtypeStruct((n_rows, d_model), _F32),
        scratch_shapes=list(scratch_shapes),
        compiler_params=pltpu.CompilerParams(
            dimension_semantics=("arbitrary",), vmem_limit_bytes=VMEM_LIMIT_BYTES),
    )(*row_inputs, *params)


def _row(v):
    return v.reshape(1, -1)


def kernel(x, p, ffn1_pre_g, ffn1_w_gate, ffn1_w_up, ffn1_w_down, ffn1_post_g, mix_pre_g, w_in, pool_w, pool_scale, w_pool_out, sgu_ln_g, sgu_ln_b, sgu_w_s, sgu_b_s, w_sgu_out, conv_dw_k, conv_dw_b, conv_ln_g, conv_ln_b, w_conv_out, w_out, mix_post_g, ffn2_pre_g, ffn2_w_gate, ffn2_w_up, ffn2_w_down, ffn2_post_g, ple_w_proj, ple_pre_g, ple_w_gate, ple_post_g):
    batch, seq, d_model = x.shape
    depth = p.shape[0]
    n_rows = batch * seq
    tm = ROW_TILE
    width = GROUP * len(POOL_WINDOWS)
    assert seq % tm == 0 and tm % CHUNK == 0
    assert w_in.shape[2] == 5 * width + 3 * d_model
    assert pool_w.shape[1:] == (len(POOL_WINDOWS), GROUP, GROUP)
    assert sgu_w_s.shape[2:] == (CHUNK, CHUNK) and sgu_w_s.shape[1] * GROUP == width
    assert conv_dw_k.shape[1:] == (CONV_KERNEL, 1, width)

    bf = lambda w: w.astype(_BF16)
    mixer_scratch = (
        pltpu.VMEM((POOL_HALO + tm, width), _F32),
        pltpu.VMEM((CONV_HALO + tm, width), _F32),
        pltpu.VMEM((tm, width), _F32),
    )
    mixer_body = functools.partial(_mixer_kernel, tiles_per_seq=seq // tm)

    h = x.reshape(n_rows, d_model)
    for i in range(depth):
        h = _row_tiled_call(
            _ffn_kernel, "ffn", n_rows, tm, [h],
            [_row(ffn1_pre_g[i]), bf(ffn1_w_gate[i]), bf(ffn1_w_up[i]), bf(ffn1_w_down[i]),
             _row(ffn1_post_g[i])])
        b_exp = jnp.repeat(sgu_b_s[i].T, GROUP, axis=1)
        h = _row_tiled_call(
            mixer_body, "mixer", n_rows, tm, [h],
            [_row(mix_pre_g[i]), bf(w_in[i]), bf(pool_w[i]), _row(pool_scale[i]), bf(w_pool_out[i]),
             _row(sgu_ln_g[i]), _row(sgu_ln_b[i]), sgu_w_s[i], b_exp, bf(w_sgu_out[i]),
             conv_dw_k[i].reshape(CONV_KERNEL, width), _row(conv_dw_b[i]),
             _row(conv_ln_g[i]), _row(conv_ln_b[i]), bf(w_conv_out[i]), bf(w_out[i]),
             _row(mix_post_g[i])],
            scratch_shapes=mixer_scratch)
        h = _row_tiled_call(
            _ffn_ple_kernel, "ffn_ple", n_rows, tm, [h, p[i].reshape(n_rows, -1)],
            [_row(ffn2_pre_g[i]), bf(ffn2_w_gate[i]), bf(ffn2_w_up[i]), bf(ffn2_w_down[i]),
             _row(ffn2_post_g[i]), bf(ple_w_proj[i]), _row(ple_pre_g[i]), bf(ple_w_gate[i]),
             _row(ple_post_g[i])])
    return h.reshape(batch, seq, d_model)
```

```python
import functools
import math

import jax
import jax.numpy as jnp
from jax import lax
from jax.experimental import pallas as pl
from jax.experimental.pallas import tpu as pltpu

POOL_WINDOWS = (2, 4, 8, 16)
GROUP = 128
CHUNK = 128
CONV_KERNEL = 31
EPS = 1e-6

SUBLANES = 8
POOL_HALO = 16
CONV_HALO = 32
ROW_TILE = 512
FF_CHUNK = 512
VMEM_LIMIT_BYTES = 56 * 1024 * 1024

_BF16 = jnp.bfloat16
_F32 = jnp.float32


def _mm(a, w):
    return jnp.dot(a.astype(_BF16), w, preferred_element_type=_F32)


def _rms_norm(x, g):
    return x * lax.rsqrt(jnp.mean(x * x, axis=-1, keepdims=True) + EPS) * g


def _layer_norm(x, g, b):
    xc = x - jnp.mean(x, axis=-1, keepdims=True)
    return xc * lax.rsqrt(jnp.mean(xc * xc, axis=-1, keepdims=True) + EPS) * g + b


def _sigmoid(x):
    return 0.5 * jnp.tanh(0.5 * x) + 0.5


def _silu(x):
    half = 0.5 * x
    return half * (1.0 + jnp.tanh(half))


def _gelu(x):
    return 0.5 * x * (1.0 + lax.erf(x * math.sqrt(0.5)))


def _swiglu_half_step(h, pre_g, wg_ref, wu_ref, wd_ref, post_g):
    x = _rms_norm(h, pre_g).astype(_BF16)
    d_ff = wg_ref.shape[1]
    acc = jnp.zeros(h.shape, _F32)
    for c0 in range(0, d_ff, FF_CHUNK):
        c1 = min(c0 + FF_CHUNK, d_ff)
        g = _mm(x, wg_ref[:, c0:c1])
        u = _mm(x, wu_ref[:, c0:c1])
        acc = acc + _mm(_silu(g) * u, wd_ref[c0:c1, :])
    return h + 0.5 * _rms_norm(acc, post_g)


def _ffn_kernel(h_ref, pre_g, wg, wu, wd, post_g, o_ref):
    o_ref[...] = _swiglu_half_step(h_ref[...], pre_g[...], wg, wu, wd, post_g[...])


def _ffn_ple_kernel(h_ref, p_ref, pre_g, wg, wu, wd, post_g,
                    w_proj, ple_pre_g, w_gate, ple_post_g, o_ref):
    h = _swiglu_half_step(h_ref[...], pre_g[...], wg, wu, wd, post_g[...])
    e = _mm(p_ref[...], w_proj[...])
    g = _sigmoid(_mm(_rms_norm(h, ple_pre_g[...]), w_gate[...]))
    o_ref[...] = h + _rms_norm(g * e, ple_post_g[...])


def _shifted_rows(buf_ref, halo, row0, nrows, cols, max_shift):
    reach = (max_shift // SUBLANES) * SUBLANES
    base = halo + row0 - reach - SUBLANES
    arr = buf_ref[base:base + nrows + reach + SUBLANES, cols]
    rolled = {0: arr}

    def tap(s):
        r, q8 = s % SUBLANES, (s // SUBLANES) * SUBLANES
        if r not in rolled:
            rolled[r] = pltpu.roll(arr, r, axis=0)
        lo = reach + SUBLANES - q8
        return rolled[r][lo:lo + nrows]

    return tap


def _mixer_kernel(h_ref, pre_g, w_in, pool_w, pool_scale, w_pool_out,
                  sgu_ln_g, sgu_ln_b, sgu_w_s, sgu_b_exp, w_sgu_out,
                  dw_k, dw_b, conv_ln_g, conv_ln_b, w_conv_out, w_out, post_g,
                  o_ref, pool_buf, conv_buf, conv_out, *, tiles_per_seq):
    tm = h_ref.shape[0]
    width = GROUP * len(POOL_WINDOWS)
    seq_tile = pl.program_id(0) % tiles_per_seq

    @pl.when(seq_tile == 0)
    def _():
        pool_buf[0:POOL_HALO, :] = jnp.zeros((POOL_HALO, width), _F32)
        conv_buf[0:CONV_HALO, :] = jnp.zeros((CONV_HALO, width), _F32)

    h = h_ref[...]
    d_model = h.shape[1]
    n = _rms_norm(h, pre_g[...]).astype(_BF16)
    off = [i * width for i in range(6)]

    def proj(c0, c1):
        return jnp.dot(n, w_in[:, c0:c1], preferred_element_type=_F32)

    z_a = proj(off[3], off[4])
    z_b = proj(off[4], off[5])
    z_pool = proj(off[0], off[1])
    z_u = proj(off[1], off[2])
    z_v = proj(off[2], off[3])
    z_gate = [proj(off[5] + k * d_model, off[5] + (k + 1) * d_model) for k in range(3)]

    conv_buf[CONV_HALO:CONV_HALO + tm, :] = z_a * _sigmoid(z_b)
    for row0 in range(0, tm, CHUNK):
        for c0 in range(0, width, GROUP):
            cols = slice(c0, c0 + GROUP)
            tap = _shifted_rows(conv_buf, CONV_HALO, row0, CHUNK, cols, CONV_KERNEL - 1)
            acc = None
            for k in range(CONV_KERNEL):
                term = tap(CONV_KERNEL - 1 - k) * dw_k[k:k + 1, cols]
                acc = term if acc is None else acc + term
            conv_out[row0:row0 + CHUNK, cols] = acc
    conv_buf[0:CONV_HALO, :] = conv_buf[tm:tm + CONV_HALO, :]
    y = _layer_norm(conv_out[...] + dw_b[...], conv_ln_g[...], conv_ln_b[...])
    conv_act = _silu(y).astype(_BF16)

    pool_buf[POOL_HALO:POOL_HALO + tm, :] = z_pool
    pos = seq_tile * tm + lax.broadcasted_iota(jnp.int32, (CHUNK, GROUP), 0)
    pooled = []
    for gi, w in enumerate(POOL_WINDOWS):
        cols = slice(gi * GROUP, (gi + 1) * GROUP)
        blocks = []
        for row0 in range(0, tm, CHUNK):
            ext = pool_buf[row0:row0 + POOL_HALO + CHUNK, cols]
            run, lead, span = ext, 0, 1
            while span < w:
                if span % SUBLANES:
                    run = run + pltpu.roll(run, span, axis=0)
                else:
                    run, lead = run[span:] + run[:-span], lead + span
                span *= 2
            count = jnp.minimum(pos + (row0 + 1), w).astype(_F32)
            blocks.append(run[POOL_HALO - lead:] / count - ext[POOL_HALO:])
        pooled.append(jnp.concatenate(blocks, axis=0).astype(_BF16))
    pool_buf[0:POOL_HALO, :] = pool_buf[tm:tm + POOL_HALO, :]

    u = _gelu(z_u)
    v = _layer_norm(_gelu(z_v), sgu_ln_g[...], sgu_ln_b[...]).astype(_BF16)
    causal = (lax.broadcasted_iota(jnp.int32, (CHUNK, CHUNK), 0)
              >= lax.broadcasted_iota(jnp.int32, (CHUNK, CHUNK), 1))
    heads = sgu_w_s.shape[0]
    w_s = [jnp.where(causal, sgu_w_s[hd], 0.0).astype(_BF16) for hd in range(heads)]

    mixed = jnp.concatenate(
        [_mm(pooled[gi], pool_w[gi]) for gi in range(len(POOL_WINDOWS))], axis=1)
    y_pool = _mm(mixed * pool_scale[...], w_pool_out[...])
    bias = sgu_b_exp[...]
    rows = []
    for row0 in range(0, tm, CHUNK):
        rows.append(jnp.concatenate(
            [_mm(w_s[hd], v[row0:row0 + CHUNK, hd * GROUP:(hd + 1) * GROUP])
             for hd in range(heads)], axis=1) + bias)
    y_sgu = _mm(u * jnp.concatenate(rows, axis=0), w_sgu_out[...])
    y_conv = _mm(conv_act, w_conv_out[...])
    merged = (_sigmoid(z_gate[0]) * y_pool + _sigmoid(z_gate[1]) * y_sgu
              + _sigmoid(z_gate[2]) * y_conv)

    o_ref[...] = h + _rms_norm(_mm(merged, w_out[...]), post_g[...])


def _row_spec(tm, cols):
    return pl.BlockSpec((tm, cols), lambda i: (i, 0))


def _whole_spec(arr):
    zeros = (0,) * arr.ndim
    return pl.BlockSpec(arr.shape, lambda i: zeros)


def _layer_row_spec(tm, cols, layer):
    return pl.BlockSpec((None, tm, cols), lambda i: (layer, i, 0))


def _row_tiled_call(body, name, n_rows, tm, row_inputs, params, scratch_shapes=(), row_specs=None):
    d_model = row_inputs[0].shape[1]
    if row_specs is None:
        row_specs = [_row_spec(tm, r.shape[1]) for r in row_inputs]
    return pl.pallas_call(
        body,
        name=name,
        grid=(n_rows // tm,),
        in_specs=list(row_specs) + [_whole_spec(p) for p in params],
        out_specs=_row_spec(tm, d_model),
        out_shape=jax.ShapeDtypeStruct((n_rows, d_model), _F32),
        scratch_shapes=list(scratch_shapes),
        compiler_params=pltpu.CompilerParams(
            dimension_semantics=("arbitrary",), vmem_limit_bytes=VMEM_LIMIT_BYTES),
    )(*row_inputs, *params)


def _row(v):
    return v.reshape(1, -1)


def kernel(x, p, ffn1_pre_g, ffn1_w_gate, ffn1_w_up, ffn1_w_down, ffn1_post_g, mix_pre_g, w_in, pool_w, pool_scale, w_pool_out, sgu_ln_g, sgu_ln_b, sgu_w_s, sgu_b_s, w_sgu_out, conv_dw_k, conv_dw_b, conv_ln_g, conv_ln_b, w_conv_out, w_out, mix_post_g, ffn2_pre_g, ffn2_w_gate, ffn2_w_up, ffn2_w_down, ffn2_post_g, ple_w_proj, ple_pre_g, ple_w_gate, ple_post_g):
    batch, seq, d_model = x.shape
    depth = p.shape[0]
    n_rows = batch * seq
    tm = ROW_TILE
    width = GROUP * len(POOL_WINDOWS)
    assert seq % tm == 0 and tm % CHUNK == 0
    assert w_in.shape[2] == 5 * width + 3 * d_model
    assert pool_w.shape[1:] == (len(POOL_WINDOWS), GROUP, GROUP)
    assert sgu_w_s.shape[2:] == (CHUNK, CHUNK) and sgu_w_s.shape[1] * GROUP == width
    assert conv_dw_k.shape[1:] == (CONV_KERNEL, 1, width)

    bf = lambda w: w.astype(_BF16)
    mixer_scratch = (
        pltpu.VMEM((POOL_HALO + tm, width), _F32),
        pltpu.VMEM((CONV_HALO + tm, width), _F32),
        pltpu.VMEM((tm, width), _F32),
    )
    mixer_body = functools.partial(_mixer_kernel, tiles_per_seq=seq // tm)

    h = x.reshape(n_rows, d_model)
    p_rows = p.reshape(depth, n_rows, p.shape[3])
    for i in range(depth):
        h = _row_tiled_call(
            _ffn_kernel, "ffn", n_rows, tm, [h],
            [_row(ffn1_pre_g[i]), bf(ffn1_w_gate[i]), bf(ffn1_w_up[i]), bf(ffn1_w_down[i]),
             _row(ffn1_post_g[i])])
        b_exp = jnp.repeat(sgu_b_s[i].T, GROUP, axis=1)
        h = _row_tiled_call(
            mixer_body, "mixer", n_rows, tm, [h],
            [_row(mix_pre_g[i]), bf(w_in[i]), bf(pool_w[i]), _row(pool_scale[i]), bf(w_pool_out[i]),
             _row(sgu_ln_g[i]), _row(sgu_ln_b[i]), sgu_w_s[i], b_exp, bf(w_sgu_out[i]),
             conv_dw_k[i].reshape(CONV_KERNEL, width), _row(conv_dw_b[i]),
             _row(conv_ln_g[i]), _row(conv_ln_b[i]), bf(w_conv_out[i]), bf(w_out[i]),
             _row(mix_post_g[i])],
            scratch_shapes=mixer_scratch)
        h = _row_tiled_call(
            _ffn_ple_kernel, "ffn_ple", n_rows, tm, [h, p_rows],
            [_row(ffn2_pre_g[i]), bf(ffn2_w_gate[i]), bf(ffn2_w_up[i]), bf(ffn2_w_down[i]),
             _row(ffn2_post_g[i]), bf(ple_w_proj[i]), _row(ple_pre_g[i]), bf(ple_w_gate[i]),
             _row(ple_post_g[i])],
            row_specs=[_row_spec(tm, d_model), _layer_row_spec(tm, p_rows.shape[2], i)])
    return h.reshape(batch, seq, d_model)
```

```python
import functools
import math

import jax
import jax.numpy as jnp
from jax import lax
from jax.experimental import pallas as pl
from jax.experimental.pallas import tpu as pltpu

POOL_WINDOWS = (2, 4, 8, 16)
GROUP = 128
CHUNK = 128
CONV_KERNEL = 31
EPS = 1e-6

SUBLANES = 8
POOL_HALO = 16
CONV_HALO = 32
CONV_ROWS = 128
ROW_TILE = 512
FFN_PARTS = 2
FF_CHUNK = 512
VMEM_LIMIT_BYTES = 56 * 1024 * 1024

_BF16 = jnp.bfloat16
_F32 = jnp.float32


def _mm(a, w):
    return jnp.dot(a.astype(_BF16), w, preferred_element_type=_F32)


def _rms_norm(x, g):
    return x * lax.rsqrt(jnp.mean(x * x, axis=-1, keepdims=True) + EPS) * g


def _layer_norm(x, g, b):
    xc = x - jnp.mean(x, axis=-1, keepdims=True)
    return xc * lax.rsqrt(jnp.mean(xc * xc, axis=-1, keepdims=True) + EPS) * g + b


def _sigmoid(x):
    return 0.5 * jnp.tanh(0.5 * x) + 0.5


def _silu(x):
    half = 0.5 * x
    return half * (1.0 + jnp.tanh(half))


def _gelu(x):
    return 0.5 * x * (1.0 + lax.erf(x * math.sqrt(0.5)))


def _swiglu_rows(h_ref, o_ref, pre_g, wg_ref, wu_ref, wd_ref, post_g, finish=None):
    parts = [slice(r0, r0 + ROW_TILE) for r0 in range(0, h_ref.shape[0], ROW_TILE)]
    hs = [h_ref[rows, :] for rows in parts]
    xs = [_rms_norm(h, pre_g).astype(_BF16) for h in hs]
    accs = [jnp.zeros(h.shape, _F32) for h in hs]
    d_ff = wg_ref.shape[1]
    chunks = [(c0, min(c0 + FF_CHUNK, d_ff)) for c0 in range(0, d_ff, FF_CHUNK)]
    for k in range(len(chunks) + len(parts) - 1):
        live = [i for i in range(len(parts)) if 0 <= k - i < len(chunks)]
        for i in live:
            c0, c1 = chunks[k - i]
            g = _mm(xs[i], wg_ref[:, c0:c1])
            u = _mm(xs[i], wu_ref[:, c0:c1])
            accs[i] = accs[i] + _mm(_silu(g) * u, wd_ref[c0:c1, :])
        for i in live:
            if k - i == len(chunks) - 1:
                h = hs[i] + 0.5 * _rms_norm(accs[i], post_g)
                o_ref[parts[i], :] = h if finish is None else finish(parts[i], h)


def _ffn_kernel(h_ref, pre_g, wg, wu, wd, post_g, o_ref):
    _swiglu_rows(h_ref, o_ref, pre_g[...], wg, wu, wd, post_g[...])


def _ffn_ple_kernel(h_ref, p_ref, pre_g, wg, wu, wd, post_g,
                    w_proj, ple_pre_g, w_gate, ple_post_g, o_ref):
    def gated_embedding(rows, h):
        e = _mm(p_ref[rows, :], w_proj[...])
        g = _sigmoid(_mm(_rms_norm(h, ple_pre_g[...]), w_gate[...]))
        return h + _rms_norm(g * e, ple_post_g[...])

    _swiglu_rows(h_ref, o_ref, pre_g[...], wg, wu, wd, post_g[...], finish=gated_embedding)


def _shifted_rows(buf_ref, halo, row0, nrows, cols, max_shift):
    reach = (max_shift // SUBLANES) * SUBLANES
    base = halo + row0 - reach - SUBLANES
    arr = buf_ref[base:base + nrows + reach + SUBLANES, cols]
    rolled = {0: arr}

    def tap(s):
        r, q8 = s % SUBLANES, (s // SUBLANES) * SUBLANES
        if r not in rolled:
            rolled[r] = pltpu.roll(arr, r, axis=0)
        lo = reach + SUBLANES - q8
        return rolled[r][lo:lo + nrows]

    return tap


def _mixer_kernel(h_ref, pre_g, w_in, pool_w, pool_scale, w_pool_out,
                  sgu_ln_g, sgu_ln_b, sgu_w_s, sgu_b_exp, w_sgu_out,
                  dw_k, dw_b, conv_ln_g, conv_ln_b, w_conv_out, w_out, post_g,
                  o_ref, pool_buf, conv_buf, conv_out, *, tiles_per_seq):
    tm = h_ref.shape[0]
    width = GROUP * len(POOL_WINDOWS)
    seq_tile = pl.program_id(0) % tiles_per_seq

    @pl.when(seq_tile == 0)
    def _():
        pool_buf[0:POOL_HALO, :] = jnp.zeros((POOL_HALO, width), _F32)
        conv_buf[0:CONV_HALO, :] = jnp.zeros((CONV_HALO, width), _F32)

    h = h_ref[...]
    d_model = h.shape[1]
    n = _rms_norm(h, pre_g[...]).astype(_BF16)
    off = [i * width for i in range(6)]

    def proj(c0, c1):
        return jnp.dot(n, w_in[:, c0:c1], preferred_element_type=_F32)

    z_a = proj(off[3], off[4])
    z_b = proj(off[4], off[5])
    z_gate = [proj(off[5] + k * d_model, off[5] + (k + 1) * d_model) for k in range(3)]
    z_pool = proj(off[0], off[1])
    z_u = proj(off[1], off[2])
    z_v = proj(off[2], off[3])

    conv_buf[CONV_HALO:CONV_HALO + tm, :] = z_a * _sigmoid(z_b)
    for row0 in range(0, tm, CONV_ROWS):
        for c0 in range(0, width, GROUP):
            cols = slice(c0, c0 + GROUP)
            tap = _shifted_rows(conv_buf, CONV_HALO, row0, CONV_ROWS, cols, CONV_KERNEL - 1)
            acc = None
            for k in range(CONV_KERNEL):
                term = tap(CONV_KERNEL - 1 - k) * dw_k[k:k + 1, cols]
                acc = term if acc is None else acc + term
            conv_out[row0:row0 + CONV_ROWS, cols] = acc
    conv_buf[0:CONV_HALO, :] = conv_buf[tm:tm + CONV_HALO, :]
    y = _layer_norm(conv_out[...] + dw_b[...], conv_ln_g[...], conv_ln_b[...])
    conv_act = _silu(y).astype(_BF16)

    pool_buf[POOL_HALO:POOL_HALO + tm, :] = z_pool
    pos = seq_tile * tm + lax.broadcasted_iota(jnp.int32, (CHUNK, GROUP), 0)
    pooled = []
    for gi, w in enumerate(POOL_WINDOWS):
        cols = slice(gi * GROUP, (gi + 1) * GROUP)
        blocks = []
        for row0 in range(0, tm, CHUNK):
            ext = pool_buf[row0:row0 + POOL_HALO + CHUNK, cols]
            run, lead, span = ext, 0, 1
            while span < w:
                if span % SUBLANES:
                    run = run + pltpu.roll(run, span, axis=0)
                else:
                    run, lead = run[span:] + run[:-span], lead + span
                span *= 2
            count = jnp.minimum(pos + (row0 + 1), w).astype(_F32)
            blocks.append(run[POOL_HALO - lead:] / count - ext[POOL_HALO:])
        pooled.append(jnp.concatenate(blocks, axis=0).astype(_BF16))
    pool_buf[0:POOL_HALO, :] = pool_buf[tm:tm + POOL_HALO, :]

    u = _gelu(z_u)
    v = _layer_norm(_gelu(z_v), sgu_ln_g[...], sgu_ln_b[...]).astype(_BF16)
    causal = (lax.broadcasted_iota(jnp.int32, (CHUNK, CHUNK), 0)
              >= lax.broadcasted_iota(jnp.int32, (CHUNK, CHUNK), 1))
    heads = sgu_w_s.shape[0]
    w_s = [jnp.where(causal, sgu_w_s[hd], 0.0).astype(_BF16) for hd in range(heads)]

    mixed = jnp.concatenate(
        [_mm(pooled[gi], pool_w[gi]) for gi in range(len(POOL_WINDOWS))], axis=1)
    y_pool = _mm(mixed * pool_scale[...], w_pool_out[...])
    bias = sgu_b_exp[...]
    rows = []
    for row0 in range(0, tm, CHUNK):
        rows.append(jnp.concatenate(
            [_mm(w_s[hd], v[row0:row0 + CHUNK, hd * GROUP:(hd + 1) * GROUP])
             for hd in range(heads)], axis=1) + bias)
    y_sgu = _mm(u * jnp.concatenate(rows, axis=0), w_sgu_out[...])
    y_conv = _mm(conv_act, w_conv_out[...])
    merged = (_sigmoid(z_gate[0]) * y_pool + _sigmoid(z_gate[1]) * y_sgu
              + _sigmoid(z_gate[2]) * y_conv)

    o_ref[...] = h + _rms_norm(_mm(merged, w_out[...]), post_g[...])


def _row_spec(tm, cols):
    return pl.BlockSpec((tm, cols), lambda i: (i, 0))


def _whole_spec(arr):
    zeros = (0,) * arr.ndim
    return pl.BlockSpec(arr.shape, lambda i: zeros)


def _layer_row_spec(tm, cols, layer):
    return pl.BlockSpec((None, tm, cols), lambda i: (layer, i, 0))


def _row_tiled_call(body, name, n_rows, tm, row_inputs, params, scratch_shapes=(), row_specs=None):
    d_model = row_inputs[0].shape[1]
    if row_specs is None:
        row_specs = [_row_spec(tm, r.shape[1]) for r in row_inputs]
    return pl.pallas_call(
        body,
        name=name,
        grid=(n_rows // tm,),
        in_specs=list(row_specs) + [_whole_spec(p) for p in params],
        out_specs=_row_spec(tm, d_model),
        out_shape=jax.ShapeDtypeStruct((n_rows, d_model), _F32),
        scratch_shapes=list(scratch_shapes),
        compiler_params=pltpu.CompilerParams(
            dimension_semantics=("arbitrary",), vmem_limit_bytes=VMEM_LIMIT_BYTES),
    )(*row_inputs, *params)


def _row(v):
    return v.reshape(1, -1)


def kernel(x, p, ffn1_pre_g, ffn1_w_gate, ffn1_w_up, ffn1_w_down, ffn1_post_g, mix_pre_g, w_in, pool_w, pool_scale, w_pool_out, sgu_ln_g, sgu_ln_b, sgu_w_s, sgu_b_s, w_sgu_out, conv_dw_k, conv_dw_b, conv_ln_g, conv_ln_b, w_conv_out, w_out, mix_post_g, ffn2_pre_g, ffn2_w_gate, ffn2_w_up, ffn2_w_down, ffn2_post_g, ple_w_proj, ple_pre_g, ple_w_gate, ple_post_g):
    batch, seq, d_model = x.shape
    depth = p.shape[0]
    n_rows = batch * seq
    tm = ROW_TILE
    ffn_tm = FFN_PARTS * ROW_TILE
    width = GROUP * len(POOL_WINDOWS)
    assert seq % tm == 0 and tm % CHUNK == 0 and n_rows % ffn_tm == 0
    assert w_in.shape[2] == 5 * width + 3 * d_model
    assert pool_w.shape[1:] == (len(POOL_WINDOWS), GROUP, GROUP)
    assert sgu_w_s.shape[2:] == (CHUNK, CHUNK) and sgu_w_s.shape[1] * GROUP == width
    assert conv_dw_k.shape[1:] == (CONV_KERNEL, 1, width)

    bf = lambda w: w.astype(_BF16)
    mixer_scratch = (
        pltpu.VMEM((POOL_HALO + tm, width), _F32),
        pltpu.VMEM((CONV_HALO + tm, width), _F32),
        pltpu.VMEM((tm, width), _F32),
    )
    mixer_body = functools.partial(_mixer_kernel, tiles_per_seq=seq // tm)

    h = x.reshape(n_rows, d_model)
    p_rows = p.reshape(depth, n_rows, p.shape[3])
    for i in range(depth):
        h = _row_tiled_call(
            _ffn_kernel, "ffn", n_rows, ffn_tm, [h],
            [_row(ffn1_pre_g[i]), bf(ffn1_w_gate[i]), bf(ffn1_w_up[i]), bf(ffn1_w_down[i]),
             _row(ffn1_post_g[i])])
        b_exp = jnp.repeat(sgu_b_s[i].T, GROUP, axis=1)
        h = _row_tiled_call(
            mixer_body, "mixer", n_rows, tm, [h],
            [_row(mix_pre_g[i]), bf(w_in[i]), bf(pool_w[i]), _row(pool_scale[i]), bf(w_pool_out[i]),
             _row(sgu_ln_g[i]), _row(sgu_ln_b[i]), sgu_w_s[i], b_exp, bf(w_sgu_out[i]),
             conv_dw_k[i].reshape(CONV_KERNEL, width), _row(conv_dw_b[i]),
             _row(conv_ln_g[i]), _row(conv_ln_b[i]), bf(w_conv_out[i]), bf(w_out[i]),
             _row(mix_post_g[i])],
            scratch_shapes=mixer_scratch)
        h = _row_tiled_call(
            _ffn_ple_kernel, "ffn_ple", n_rows, ffn_tm, [h, p_rows],
            [_row(ffn2_pre_g[i]), bf(ffn2_w_gate[i]), bf(ffn2_w_up[i]), bf(ffn2_w_down[i]),
             _row(ffn2_post_g[i]), bf(ple_w_proj[i]), _row(ple_pre_g[i]), bf(ple_w_gate[i]),
             _row(ple_post_g[i])],
            row_specs=[_row_spec(ffn_tm, d_model), _layer_row_spec(ffn_tm, p_rows.shape[2], i)])
    return h.reshape(batch, seq, d_model)
```

```python
import functools
import math

import jax
import jax.numpy as jnp
from jax import lax
from jax.experimental import pallas as pl
from jax.experimental.pallas import tpu as pltpu

POOL_WINDOWS = (2, 4, 8, 16)
GROUP = 128
CHUNK = 128
CONV_KERNEL = 31
EPS = 1e-6

SUBLANES = 8
POOL_HALO = 16
CONV_HALO = 32
CONV_ROWS = 128
ROW_TILE = 512
FFN_PARTS = 2
FF_CHUNK = 512
VMEM_LIMIT_BYTES = 56 * 1024 * 1024

_BF16 = jnp.bfloat16
_F32 = jnp.float32


def _mm(a, w):
    return jnp.dot(a.astype(_BF16), w, preferred_element_type=_F32)


def _rms_norm(x, g):
    return x * lax.rsqrt(jnp.mean(x * x, axis=-1, keepdims=True) + EPS) * g


def _layer_norm(x, g, b):
    xc = x - jnp.mean(x, axis=-1, keepdims=True)
    return xc * lax.rsqrt(jnp.mean(xc * xc, axis=-1, keepdims=True) + EPS) * g + b


def _sigmoid(x):
    return 0.5 * jnp.tanh(0.5 * x) + 0.5


def _silu(x):
    half = 0.5 * x
    return half * (1.0 + jnp.tanh(half))


def _gelu(x):
    return 0.5 * x * (1.0 + lax.erf(x * math.sqrt(0.5)))


def _swiglu_rows(h_ref, h_next_ref, x_carry, o_ref, pre_g, wg_ref, wu_ref, wd_ref, post_g,
                 finish=None):
    parts = [slice(r0, r0 + ROW_TILE) for r0 in range(0, h_ref.shape[0], ROW_TILE)]
    hs = [h_ref[rows, :] for rows in parts]

    @pl.when(pl.program_id(0) == 0)
    def _():
        x_carry[...] = _rms_norm(h_ref[parts[0], :], pre_g).astype(_BF16)

    xs = [x_carry[...]] + [_rms_norm(h, pre_g).astype(_BF16) for h in hs[1:]]
    accs = [jnp.zeros(h.shape, _F32) for h in hs]
    d_ff = wg_ref.shape[1]
    chunks = [(c0, min(c0 + FF_CHUNK, d_ff)) for c0 in range(0, d_ff, FF_CHUNK)]
    for k in range(len(chunks) + len(parts) - 1):
        live = [i for i in range(len(parts)) if 0 <= k - i < len(chunks)]
        for i in live:
            c0, c1 = chunks[k - i]
            g = _mm(xs[i], wg_ref[:, c0:c1])
            u = _mm(xs[i], wu_ref[:, c0:c1])
            accs[i] = accs[i] + _mm(_silu(g) * u, wd_ref[c0:c1, :])
        for i in live:
            if k - i == len(chunks) - 1:
                h = hs[i] + 0.5 * _rms_norm(accs[i], post_g)
                o_ref[parts[i], :] = h if finish is None else finish(parts[i], h)
    x_carry[...] = _rms_norm(h_next_ref[...], pre_g).astype(_BF16)


def _ffn_kernel(h_ref, h_next_ref, pre_g, wg, wu, wd, post_g, o_ref, x_carry):
    _swiglu_rows(h_ref, h_next_ref, x_carry, o_ref, pre_g[...], wg, wu, wd, post_g[...])


def _ffn_ple_kernel(h_ref, h_next_ref, p_ref, pre_g, wg, wu, wd, post_g,
                    w_proj, ple_pre_g, w_gate, ple_post_g, o_ref, x_carry):
    def gated_embedding(rows, h):
        e = _mm(p_ref[rows, :], w_proj[...])
        g = _sigmoid(_mm(_rms_norm(h, ple_pre_g[...]), w_gate[...]))
        return h + _rms_norm(g * e, ple_post_g[...])

    _swiglu_rows(h_ref, h_next_ref, x_carry, o_ref, pre_g[...], wg, wu, wd, post_g[...],
                 finish=gated_embedding)


def _shifted_rows(buf_ref, halo, row0, nrows, cols, max_shift):
    reach = (max_shift // SUBLANES) * SUBLANES
    base = halo + row0 - reach - SUBLANES
    arr = buf_ref[base:base + nrows + reach + SUBLANES, cols]
    rolled = {0: arr}

    def tap(s):
        r, q8 = s % SUBLANES, (s // SUBLANES) * SUBLANES
        if r not in rolled:
            rolled[r] = pltpu.roll(arr, r, axis=0)
        lo = reach + SUBLANES - q8
        return rolled[r][lo:lo + nrows]

    return tap


def _mixer_kernel(h_ref, pre_g, w_in, pool_w, pool_scale, w_pool_out,
                  sgu_ln_g, sgu_ln_b, sgu_w_s, sgu_b_exp, w_sgu_out,
                  dw_k, dw_b, conv_ln_g, conv_ln_b, w_conv_out, w_out, post_g,
                  o_ref, pool_buf, conv_buf, conv_out, *, tiles_per_seq):
    tm = h_ref.shape[0]
    width = GROUP * len(POOL_WINDOWS)
    seq_tile = pl.program_id(0) % tiles_per_seq

    @pl.when(seq_tile == 0)
    def _():
        pool_buf[0:POOL_HALO, :] = jnp.zeros((POOL_HALO, width), _F32)
        conv_buf[0:CONV_HALO, :] = jnp.zeros((CONV_HALO, width), _F32)

    h = h_ref[...]
    d_model = h.shape[1]
    n = _rms_norm(h, pre_g[...]).astype(_BF16)
    off = [i * width for i in range(6)]

    def proj(c0, c1):
        return jnp.dot(n, w_in[:, c0:c1], preferred_element_type=_F32)

    z_a = proj(off[3], off[4])
    z_b = proj(off[4], off[5])
    z_gate = [proj(off[5] + k * d_model, off[5] + (k + 1) * d_model) for k in range(3)]
    z_pool = proj(off[0], off[1])
    z_u = proj(off[1], off[2])
    z_v = proj(off[2], off[3])

    conv_buf[CONV_HALO:CONV_HALO + tm, :] = z_a * _sigmoid(z_b)
    for row0 in range(0, tm, CONV_ROWS):
        for c0 in range(0, width, GROUP):
            cols = slice(c0, c0 + GROUP)
            tap = _shifted_rows(conv_buf, CONV_HALO, row0, CONV_ROWS, cols, CONV_KERNEL - 1)
            acc = None
            for k in range(CONV_KERNEL):
                term = tap(CONV_KERNEL - 1 - k) * dw_k[k:k + 1, cols]
                acc = term if acc is None else acc + term
            conv_out[row0:row0 + CONV_ROWS, cols] = acc
    conv_buf[0:CONV_HALO, :] = conv_buf[tm:tm + CONV_HALO, :]
    y = _layer_norm(conv_out[...] + dw_b[...], conv_ln_g[...], conv_ln_b[...])
    conv_act = _silu(y).astype(_BF16)

    pool_buf[POOL_HALO:POOL_HALO + tm, :] = z_pool
    pos = seq_tile * tm + lax.broadcasted_iota(jnp.int32, (CHUNK, GROUP), 0)
    pooled = []
    for gi, w in enumerate(POOL_WINDOWS):
        cols = slice(gi * GROUP, (gi + 1) * GROUP)
        blocks = []
        for row0 in range(0, tm, CHUNK):
            ext = pool_buf[row0:row0 + POOL_HALO + CHUNK, cols]
            run, lead, span = ext, 0, 1
            while span < w:
                if span % SUBLANES:
                    run = run + pltpu.roll(run, span, axis=0)
                else:
                    run, lead = run[span:] + run[:-span], lead + span
                span *= 2
            count = jnp.minimum(pos + (row0 + 1), w).astype(_F32)
            blocks.append(run[POOL_HALO - lead:] / count - ext[POOL_HALO:])
        pooled.append(jnp.concatenate(blocks, axis=0).astype(_BF16))
    pool_buf[0:POOL_HALO, :] = pool_buf[tm:tm + POOL_HALO, :]

    u = _gelu(z_u)
    v = _layer_norm(_gelu(z_v), sgu_ln_g[...], sgu_ln_b[...]).astype(_BF16)
    causal = (lax.broadcasted_iota(jnp.int32, (CHUNK, CHUNK), 0)
              >= lax.broadcasted_iota(jnp.int32, (CHUNK, CHUNK), 1))
    heads = sgu_w_s.shape[0]
    w_s = [jnp.where(causal, sgu_w_s[hd], 0.0).astype(_BF16) for hd in range(heads)]

    mixed = jnp.concatenate(
        [_mm(pooled[gi], pool_w[gi]) for gi in range(len(POOL_WINDOWS))], axis=1)
    y_pool = _mm(mixed * pool_scale[...], w_pool_out[...])
    bias = sgu_b_exp[...]
    rows = []
    for row0 in range(0, tm, CHUNK):
        rows.append(jnp.concatenate(
            [_mm(w_s[hd], v[row0:row0 + CHUNK, hd * GROUP:(hd + 1) * GROUP])
             for hd in range(heads)], axis=1) + bias)
    y_sgu = _mm(u * jnp.concatenate(rows, axis=0), w_sgu_out[...])
    y_conv = _mm(conv_act, w_conv_out[...])
    merged = (_sigmoid(z_gate[0]) * y_pool + _sigmoid(z_gate[1]) * y_sgu
              + _sigmoid(z_gate[2]) * y_conv)

    o_ref[...] = h + _rms_norm(_mm(merged, w_out[...]), post_g[...])


def _row_spec(tm, cols):
    return pl.BlockSpec((tm, cols), lambda i: (i, 0))


def _whole_spec(arr):
    zeros = (0,) * arr.ndim
    return pl.BlockSpec(arr.shape, lambda i: zeros)


def _layer_row_spec(tm, cols, layer):
    return pl.BlockSpec((None, tm, cols), lambda i: (layer, i, 0))


def _row_tiled_call(body, name, n_rows, tm, row_inputs, params, scratch_shapes=(), row_specs=None):
    d_model = row_inputs[0].shape[1]
    if row_specs is None:
        row_specs = [_row_spec(tm, r.shape[1]) for r in row_inputs]
    return pl.pallas_call(
        body,
        name=name,
        grid=(n_rows // tm,),
        in_specs=list(row_specs) + [_whole_spec(p) for p in params],
        out_specs=_row_spec(tm, d_model),
        out_shape=jax.ShapeDtypeStruct((n_rows, d_model), _F32),
        scratch_shapes=list(scratch_shapes),
        compiler_params=pltpu.CompilerParams(
            dimension_semantics=("arbitrary",), vmem_limit_bytes=VMEM_LIMIT_BYTES),
    )(*row_inputs, *params)


def _row(v):
    return v.reshape(1, -1)


def kernel(x, p, ffn1_pre_g, ffn1_w_gate, ffn1_w_up, ffn1_w_down, ffn1_post_g, mix_pre_g, w_in, pool_w, pool_scale, w_pool_out, sgu_ln_g, sgu_ln_b, sgu_w_s, sgu_b_s, w_sgu_out, conv_dw_k, conv_dw_b, conv_ln_g, conv_ln_b, w_conv_out, w_out, mix_post_g, ffn2_pre_g, ffn2_w_gate, ffn2_w_up, ffn2_w_down, ffn2_post_g, ple_w_proj, ple_pre_g, ple_w_gate, ple_post_g):
    batch, seq, d_model = x.shape
    depth = p.shape[0]
    n_rows = batch * seq
    tm = ROW_TILE
    ffn_tm = FFN_PARTS * ROW_TILE
    width = GROUP * len(POOL_WINDOWS)
    assert seq % tm == 0 and tm % CHUNK == 0 and n_rows % ffn_tm == 0
    assert w_in.shape[2] == 5 * width + 3 * d_model
    assert pool_w.shape[1:] == (len(POOL_WINDOWS), GROUP, GROUP)
    assert sgu_w_s.shape[2:] == (CHUNK, CHUNK) and sgu_w_s.shape[1] * GROUP == width
    assert conv_dw_k.shape[1:] == (CONV_KERNEL, 1, width)

    bf = lambda w: w.astype(_BF16)
    mixer_scratch = (
        pltpu.VMEM((POOL_HALO + tm, width), _F32),
        pltpu.VMEM((CONV_HALO + tm, width), _F32),
        pltpu.VMEM((tm, width), _F32),
    )
    mixer_body = functools.partial(_mixer_kernel, tiles_per_seq=seq // tm)
    last_part = n_rows // tm - 1
    next_part_spec = pl.BlockSpec(
        (tm, d_model), lambda j: (jnp.minimum((j + 1) * FFN_PARTS, last_part), 0))
    ffn_scratch = (pltpu.VMEM((tm, d_model), _BF16),)

    h = x.reshape(n_rows, d_model)
    p_rows = p.reshape(depth, n_rows, p.shape[3])
    for i in range(depth):
        h = _row_tiled_call(
            _ffn_kernel, "ffn", n_rows, ffn_tm, [h, h],
            [_row(ffn1_pre_g[i]), bf(ffn1_w_gate[i]), bf(ffn1_w_up[i]), bf(ffn1_w_down[i]),
             _row(ffn1_post_g[i])],
            scratch_shapes=ffn_scratch,
            row_specs=[_row_spec(ffn_tm, d_model), next_part_spec])
        b_exp = jnp.repeat(sgu_b_s[i].T, GROUP, axis=1)
        h = _row_tiled_call(
            mixer_body, "mixer", n_rows, tm, [h],
            [_row(mix_pre_g[i]), bf(w_in[i]), bf(pool_w[i]), _row(pool_scale[i]), bf(w_pool_out[i]),
             _row(sgu_ln_g[i]), _row(sgu_ln_b[i]), sgu_w_s[i], b_exp, bf(w_sgu_out[i]),
             conv_dw_k[i].reshape(CONV_KERNEL, width), _row(conv_dw_b[i]),
             _row(conv_ln_g[i]), _row(conv_ln_b[i]), bf(w_conv_out[i]), bf(w_out[i]),
             _row(mix_post_g[i])],
            scratch_shapes=mixer_scratch)
        h = _row_tiled_call(
            _ffn_ple_kernel, "ffn_ple", n_rows, ffn_tm, [h, h, p_rows],
            [_row(ffn2_pre_g[i]), bf(ffn2_w_gate[i]), bf(ffn2_w_up[i]), bf(ffn2_w_down[i]),
             _row(ffn2_post_g[i]), bf(ple_w_proj[i]), _row(ple_pre_g[i]), bf(ple_w_gate[i]),
             _row(ple_post_g[i])],
            scratch_shapes=ffn_scratch,
            row_specs=[_row_spec(ffn_tm, d_model), next_part_spec,
                       _layer_row_spec(ffn_tm, p_rows.shape[2], i)])
    return h.reshape(batch, seq, d_model)
```
